```python
import math
import jax, jax.numpy as jnp
from jax import lax
import numpy as np


D_MODEL = 1024
BATCH = 32
SEQ = 256
DEPTH = 2
DEC_BATCH = 8
DEC_SEQ = 2048
PAST_LEN = 512

GRID_W = 64
HEAD_DIM = 64
A_HEADS = 8
A_KV_HEADS = 2
B_GROUPS = 4
B_GROUP_DIM = 128
B_WIDTH = B_GROUPS * B_GROUP_DIM
CHUNK = 128
C_HEADS = 16
C_KV_HEADS = 4
WINDOW = 128
Q_BLOCK = 128
FFN_DIM = 2816
CONV_W = 3
ROPE_THETA = 10000.0
EPS = 1e-6
NEG = -1e30
N_EVEN = (DEPTH + 1) // 2
N_ODD = DEPTH // 2
A_Q_W = A_HEADS * HEAD_DIM
A_KV_W = A_KV_HEADS * HEAD_DIM
EVEN_IN = A_Q_W + 2 * A_KV_W + 2 * B_WIDTH
EVEN_SPLITS = (A_Q_W, A_Q_W + A_KV_W, A_Q_W + 2 * A_KV_W, A_Q_W + 2 * A_KV_W + B_WIDTH)
EVEN_MIX = A_Q_W + B_WIDTH
C_Q_W = C_HEADS * HEAD_DIM
C_KV_W = C_KV_HEADS * HEAD_DIM
ODD_IN = C_Q_W + 2 * C_KV_W
ODD_SPLITS = (C_Q_W, C_Q_W + C_KV_W)
ODD_MIX = C_Q_W

kernel_name = "hybrid_flow_prefix_trunk_step"


def rms_norm(x, gain=None):
    xf = x.astype(jnp.float32)
    y = xf * lax.rsqrt(jnp.mean(xf * xf, axis=-1, keepdims=True) + EPS)
    if gain is not None:
        y = y * gain.astype(jnp.float32)
    return y.astype(x.dtype)


def modulation(cond, w_mod, b_mod):
    m = jax.nn.silu(cond) @ w_mod + b_mod
    return jnp.split(m[:, None, :], 6, axis=-1)


def adaln(x, shift, scale):
    return rms_norm(x) * (1 + scale) + shift


def axial_rope(n_tokens):
    n_rows = n_tokens // GRID_W
    rows = jnp.repeat(jnp.arange(n_rows), GRID_W).astype(jnp.float32)
    cols = jnp.tile(jnp.arange(GRID_W), n_rows).astype(jnp.float32)
    n_freq = HEAD_DIM // 4
    inv = ROPE_THETA ** (-jnp.arange(n_freq, dtype=jnp.float32) / n_freq)
    ang = jnp.concatenate([rows[:, None] * inv, cols[:, None] * inv], axis=-1)
    return jnp.cos(ang), jnp.sin(ang)


def apply_rope(x, cos, sin):
    xf = x.astype(jnp.float32).reshape(x.shape[:-1] + (HEAD_DIM // 2, 2))
    x0, x1 = xf[..., 0], xf[..., 1]
    cc = cos[None, :, None, :]
    ss = sin[None, :, None, :]
    out = jnp.stack([x0 * cc - x1 * ss, x0 * ss + x1 * cc], axis=-1)
    return out.reshape(x.shape).astype(x.dtype)


def attend(q, k, v, mask=None, sink=None):
    s = jnp.einsum('bqhgd,bkhd->bhgqk', q, k, preferred_element_type=jnp.float32) * (HEAD_DIM ** -0.5)
    if mask is not None:
        s = jnp.where(mask, s, NEG)
    if sink is not None:
        sk = jnp.broadcast_to(sink.astype(jnp.float32)[None, :, :, None, None], s.shape[:-1] + (1,))
        p = jax.nn.softmax(jnp.concatenate([s, sk], axis=-1), axis=-1)[..., :-1]
    else:
        p = jax.nn.softmax(s, axis=-1)
    return jnp.einsum('bhgqk,bkhd->bqhgd', p.astype(v.dtype), v)


def dense_block_attention(q, k, v, sink=None):
    b, t, hq, dh = q.shape
    hkv = k.shape[2]
    g = hq // hkv
    qb = q.reshape(b, t // Q_BLOCK, Q_BLOCK, hkv, g, dh).transpose(1, 0, 2, 3, 4, 5)
    out = lax.map(lambda qi: attend(qi, k, v, None, sink), qb)
    return out.transpose(1, 0, 2, 3, 4, 5).reshape(b, t, hq * dh)


def window_attention_with_context(q, k, v, k_ctx, v_ctx, sink):
    b, t, hq, dh = q.shape
    hkv = k.shape[2]
    g = hq // hkv
    nb = t // Q_BLOCK
    span = Q_BLOCK + 2 * WINDOW
    n_ctx = k_ctx.shape[1]
    pad = ((0, 0), (WINDOW, WINDOW), (0, 0), (0, 0))
    kp = jnp.pad(k, pad)
    vp = jnp.pad(v, pad)
    qb = q.reshape(b, nb, Q_BLOCK, hkv, g, dh).transpose(1, 0, 2, 3, 4, 5)
    ctx_mask = jnp.ones((Q_BLOCK, n_ctx), dtype=bool)

    def one_block(args):
        qi, bi = args
        start = bi * Q_BLOCK
        kw = lax.dynamic_slice_in_dim(kp, start, span, axis=1)
        vw = lax.dynamic_slice_in_dim(vp, start, span, axis=1)
        qpos = start + jnp.arange(Q_BLOCK)
        kpos = start - WINDOW + jnp.arange(span)
        local = (jnp.abs(kpos[None, :] - qpos[:, None]) <= WINDOW) & (kpos[None, :] >= 0) & (kpos[None, :] < t)
        mask = jnp.concatenate([local, ctx_mask], axis=1)
        return attend(qi, jnp.concatenate([kw, k_ctx], axis=1), jnp.concatenate([vw, v_ctx], axis=1), mask, sink)

    out = lax.map(one_block, (qb, jnp.arange(nb)))
    return out.transpose(1, 0, 2, 3, 4, 5).reshape(b, t, hq * dh)


def chunk_spatial_gate(u, v, v_gain, w_s, b_s):
    b, t, _ = u.shape
    vn = rms_norm(v, v_gain).reshape(b, t // CHUNK, CHUNK, B_GROUPS, B_GROUP_DIM)
    mixed = jnp.einsum('gpq,bnqgc->bnpgc', w_s, vn) + b_s.T[None, None, :, :, None]
    return u * mixed.reshape(b, t, B_WIDTH)


def even_mixer(h, w_in, q_gain, k_gain, v_gain, w_s, b_s, w_out, rope=None, ctx_kv=None):
    b, t, _ = h.shape
    z = h @ w_in
    q, k, v, u_b, v_b = jnp.split(z, EVEN_SPLITS, axis=-1)
    q = rms_norm(q.reshape(b, t, A_HEADS, HEAD_DIM), q_gain)
    k = rms_norm(k.reshape(b, t, A_KV_HEADS, HEAD_DIM), k_gain)
    v = v.reshape(b, t, A_KV_HEADS, HEAD_DIM)
    if rope is not None:
        q = apply_rope(q, rope[0], rope[1])
        k_rot = apply_rope(k, rope[0], rope[1])
    else:
        k_rot = k
    if ctx_kv is not None:
        k_all = jnp.concatenate([k_rot, ctx_kv[0]], axis=1)
        v_all = jnp.concatenate([v, ctx_kv[1]], axis=1)
    else:
        k_all, v_all = k_rot, v
    attn = dense_block_attention(q, k_all, v_all)
    gate = chunk_spatial_gate(jax.nn.gelu(u_b), jax.nn.gelu(v_b), v_gain, w_s, b_s)
    return jnp.concatenate([attn, gate], axis=-1) @ w_out, k, v


def odd_mixer(h, w_in, q_gain, k_gain, sink, w_out, rope=None, ctx_kv=None):
    b, t, _ = h.shape
    z = h @ w_in
    q, k, v = jnp.split(z, ODD_SPLITS, axis=-1)
    q = rms_norm(q.reshape(b, t, C_HEADS, HEAD_DIM), q_gain)
    k = rms_norm(k.reshape(b, t, C_KV_HEADS, HEAD_DIM), k_gain)
    v = v.reshape(b, t, C_KV_HEADS, HEAD_DIM)
    sink_g = sink.reshape(C_KV_HEADS, C_HEADS // C_KV_HEADS)
    if ctx_kv is None:
        attn = dense_block_attention(q, k, v, sink_g)
    else:
        q = apply_rope(q, rope[0], rope[1])
        k_rot = apply_rope(k, rope[0], rope[1])
        attn = window_attention_with_context(q, k_rot, v, ctx_kv[0], ctx_kv[1], sink_g)
    return attn @ w_out, k, v


def conv_ffn(h, w_up, conv_w, conv_b, w_down):
    t = h.shape[1]
    z = h @ w_up
    half = CONV_W // 2
    zp = jnp.pad(z, ((0, 0), (half, half), (0, 0)))
    zc = conv_b + sum(zp[:, i:i + t] * conv_w[i] for i in range(CONV_W))
    a, g = jnp.split(zc, 2, axis=-1)
    return (a * jax.nn.silu(g)) @ w_down


def setup_inputs(seed: int = 0) -> dict:
    key = jax.random.key(seed)
    ks = jax.random.split(key, 32)
    f32 = jnp.float32
    nrm = lambda k, s, sc: jax.random.normal(k, s, f32) * sc
    return {
        "x_prompt": nrm(ks[0], (BATCH, SEQ, D_MODEL), 1.0),
        "x_sample": nrm(ks[1], (DEC_BATCH, DEC_SEQ, D_MODEL), 1.0),
        "cache_k_attn": nrm(ks[2], (DEC_BATCH, N_EVEN, PAST_LEN, A_KV_HEADS, HEAD_DIM), 1.0),
        "cache_v_attn": nrm(ks[3], (DEC_BATCH, N_EVEN, PAST_LEN, A_KV_HEADS, HEAD_DIM), 1.0),
        "cache_k_swa": nrm(ks[4], (DEC_BATCH, N_ODD, PAST_LEN, C_KV_HEADS, HEAD_DIM), 1.0),
        "cache_v_swa": nrm(ks[5], (DEC_BATCH, N_ODD, PAST_LEN, C_KV_HEADS, HEAD_DIM), 1.0),
        "c": nrm(ks[6], (DEC_BATCH, D_MODEL), 1.0),
        "c_ctx": nrm(ks[7], (D_MODEL,), 1.0),
        "w_mod": nrm(ks[8], (DEPTH, D_MODEL, 6 * D_MODEL), 0.5 * D_MODEL ** -0.5),
        "b_mod": nrm(ks[9], (DEPTH, 6 * D_MODEL), 0.02),
        "w_in_even": nrm(ks[10], (N_EVEN, D_MODEL, EVEN_IN), D_MODEL ** -0.5),
        "q_norm_a": 1.0 + nrm(ks[11], (N_EVEN, HEAD_DIM), 0.02),
        "k_norm_a": 1.0 + nrm(ks[12], (N_EVEN, HEAD_DIM), 0.02),
        "v_norm_b": 1.0 + nrm(ks[13], (N_EVEN, B_WIDTH), 0.02),
        "w_spatial": nrm(ks[14], (N_EVEN, B_GROUPS, CHUNK, CHUNK), CHUNK ** -0.5),
        "b_spatial": 1.0 + nrm(ks[15], (N_EVEN, B_GROUPS, CHUNK), 0.02),
        "w_out_even": nrm(ks[16], (N_EVEN, EVEN_MIX, D_MODEL), EVEN_MIX ** -0.5),
        "w_in_odd": nrm(ks[17], (N_ODD, D_MODEL, ODD_IN), D_MODEL ** -0.5),
        "q_norm_c": 1.0 + nrm(ks[18], (N_ODD, HEAD_DIM), 0.02),
        "k_norm_c": 1.0 + nrm(ks[19], (N_ODD, HEAD_DIM), 0.02),
        "sink_c": nrm(ks[20], (N_ODD, C_HEADS), 0.5),
        "w_out_odd": nrm(ks[21], (N_ODD, ODD_MIX, D_MODEL), ODD_MIX ** -0.5),
        "w_up": nrm(ks[22], (DEPTH, D_MODEL, 2 * FFN_DIM), D_MODEL ** -0.5),
        "conv_w": nrm(ks[23], (DEPTH, CONV_W, 2 * FFN_DIM), CONV_W ** -0.5),
        "conv_b": nrm(ks[24], (DEPTH, 2 * FFN_DIM), 0.02),
        "w_down": nrm(ks[25], (DEPTH, FFN_DIM, D_MODEL), FFN_DIM ** -0.5),
    }


def reference(x_prompt, x_sample, cache_k_attn, cache_v_attn, cache_k_swa, cache_v_swa, c, c_ctx,
              w_mod, b_mod, w_in_even, q_norm_a, k_norm_a, v_norm_b, w_spatial, b_spatial, w_out_even,
              w_in_odd, q_norm_c, k_norm_c, sink_c, w_out_odd, w_up, conv_w, conv_b, w_down):
    cos, sin = axial_rope(x_sample.shape[1])
    xc, xs = x_prompt, x_sample
    ks_a, vs_a, ks_c, vs_c = [], [], [], []
    for layer in range(DEPTH):
        j = layer // 2
        mod_c = modulation(c_ctx[None, :], w_mod[layer], b_mod[layer])
        mod_s = modulation(c, w_mod[layer], b_mod[layer])
        hc = adaln(xc, mod_c[0], mod_c[1])
        hs = adaln(xs, mod_s[0], mod_s[1])
        if layer % 2 == 0:
            out_c, kc, vc = even_mixer(hc, w_in_even[j], q_norm_a[j], k_norm_a[j], v_norm_b[j],
                                       w_spatial[j], b_spatial[j], w_out_even[j])
            ks_a.append(kc)
            vs_a.append(vc)
            out_s, _, _ = even_mixer(hs, w_in_even[j], q_norm_a[j], k_norm_a[j], v_norm_b[j],
                                     w_spatial[j], b_spatial[j], w_out_even[j],
                                     rope=(cos, sin), ctx_kv=(cache_k_attn[:, j], cache_v_attn[:, j]))
        else:
            out_c, kc, vc = odd_mixer(hc, w_in_odd[j], q_norm_c[j], k_norm_c[j], sink_c[j], w_out_odd[j])
            ks_c.append(kc)
            vs_c.append(vc)
            out_s, _, _ = odd_mixer(hs, w_in_odd[j], q_norm_c[j], k_norm_c[j], sink_c[j], w_out_odd[j],
                                    rope=(cos, sin), ctx_kv=(cache_k_swa[:, j], cache_v_swa[:, j]))
        xc = xc + mod_c[2] * out_c
        xs = xs + mod_s[2] * out_s
        xc = xc + mod_c[5] * conv_ffn(adaln(xc, mod_c[3], mod_c[4]), w_up[layer], conv_w[layer], conv_b[layer], w_down[layer])
        xs = xs + mod_s[5] * conv_ffn(adaln(xs, mod_s[3], mod_s[4]), w_up[layer], conv_w[layer], conv_b[layer], w_down[layer])
    new_k_attn = jnp.stack(ks_a, axis=1)
    new_v_attn = jnp.stack(vs_a, axis=1)
    new_k_swa = jnp.stack(ks_c, axis=1)
    new_v_swa = jnp.stack(vs_c, axis=1)
    return (xc, xs, new_k_attn, new_v_attn, new_k_swa, new_v_swa)
```

```python
import functools

import jax
import jax.numpy as jnp
from jax import lax
from jax.experimental import pallas as pl
from jax.experimental.pallas import tpu as pltpu

D_MODEL = 1024
HEAD_DIM = 64
GRID_W = 64
GROUP = 4
GROUP_W = GROUP * HEAD_DIM
A_HEADS, A_KV = 8, 2
C_HEADS, C_KV = 16, 4
B_GROUPS, B_GROUP_DIM, CHUNK = 4, 128, 128
B_WIDTH = B_GROUPS * B_GROUP_DIM
WINDOW = 128
FFN_DIM = 2816
FFN_CHUNK = 256
N_FFN_CHUNKS = FFN_DIM // FFN_CHUNK
ROPE_THETA = 10000.0
EPS = 1e-6
NEG = -1e30
N_MOD_ROWS = 16
HALO = 16
MIB = 1024 * 1024

_BF16 = jnp.bfloat16
_F32 = jnp.float32


def _dot(a, b):
    return jnp.dot(a, b, preferred_element_type=_F32)


def _dot_nt(a, b):
    return lax.dot_general(a, b, (((1,), (1,)), ((), ())), preferred_element_type=_F32)


def _rms(x):
    return x * lax.rsqrt(jnp.mean(x * x, axis=-1, keepdims=True) + EPS)


def _sigmoid(x):
    return 1.0 / (1.0 + jnp.exp(-x))


def _gelu_tanh(x):
    return 0.5 * x * (1.0 + jnp.tanh(0.7978845608028654 * (x + 0.044715 * (x * x * x))))


def _head_norm(t, head_mean):
    slab = min(t.shape[1], head_mean.shape[0])
    outs = []
    for s in range(t.shape[1] // slab):
        ts = t[:, s * slab:(s + 1) * slab]
        ms = _dot((ts * ts).astype(_BF16), head_mean[:slab, :slab])
        outs.append(ts * lax.rsqrt(ms + EPS))
    return outs[0] if len(outs) == 1 else jnp.concatenate(outs, axis=1)


def _rope(t, cos, sin_signed):
    even_lane = (lax.broadcasted_iota(jnp.int32, cos.shape, 1) & 1) == 0
    outs = []
    for s in range(t.shape[1] // 128):
        ts = t[:, s * 128:(s + 1) * 128]
        partner = jnp.where(even_lane, pltpu.roll(ts, 127, 1), pltpu.roll(ts, 1, 1))
        outs.append(ts * cos + partner * sin_signed)
    return outs[0] if len(outs) == 1 else jnp.concatenate(outs, axis=1)


def _mod_kernel(cond_ref, w_ref, b_ref, o_ref):
    cnd = cond_ref[...]
    act = cnd * _sigmoid(cnd)
    o_ref[...] = jnp.dot(act, w_ref[...], preferred_element_type=_F32,
                         precision=lax.Precision.HIGHEST) + b_ref[...]


def _modulation(cond, w_mod, b_mod):
    depth, _, n_out = w_mod.shape
    bn = D_MODEL
    out = pl.pallas_call(
        _mod_kernel,
        grid=(depth, n_out // bn),
        in_specs=[
            pl.BlockSpec((N_MOD_ROWS, D_MODEL), lambda l, n: (0, 0)),
            pl.BlockSpec((None, D_MODEL, bn), lambda l, n: (l, 0, n)),
            pl.BlockSpec((None, 1, bn), lambda l, n: (l, 0, n)),
        ],
        out_specs=pl.BlockSpec((None, N_MOD_ROWS, bn), lambda l, n: (l, 0, n)),
        out_shape=jax.ShapeDtypeStruct((depth, N_MOD_ROWS, n_out), _F32),
        compiler_params=pltpu.CompilerParams(vmem_limit_bytes=24 * MIB),
        name="modulation",
    )(cond, w_mod, b_mod.reshape(depth, 1, n_out))
    return out.reshape(depth, N_MOD_ROWS, 6, D_MODEL)


def _pre_kernel(*refs, even, latent, tm):
    it = iter(refs)
    x_ref, mod_ref, w_ref, hm_ref, qg_ref, kg_ref = (next(it) for _ in range(6))
    if even:
        vg_ref, ws_ref, bs_ref = (next(it) for _ in range(3))
    if latent:
        cos_ref, sin_ref = (next(it) for _ in range(2))
    q_ref, k_ref, v_ref = (next(it) for _ in range(3))
    if even:
        gate_ref = next(it)
    if not latent:
        kf_ref, vf_ref = (next(it) for _ in range(2))

    wq = (A_HEADS if even else C_HEADS) * HEAD_DIM
    wk = (A_KV if even else C_KV) * HEAD_DIM

    m = mod_ref[0]
    h = _rms(x_ref[...]) * (1.0 + m[1:2]) + m[0:1]
    z = _dot(h.astype(_BF16), w_ref[...])
    v = z[:, wq + wk:wq + 2 * wk]
    head_mean = hm_ref[...]
    qn = _head_norm(z[:, :wq], head_mean) * qg_ref[...]
    kn = _head_norm(z[:, wq:wq + wk], head_mean) * kg_ref[...]
    if latent:
        cos, sin_signed = cos_ref[...], sin_ref[...]
        q_ref[...] = _rope(qn, cos, sin_signed).astype(_BF16)
        k_ref[...] = _rope(kn, cos, sin_signed).astype(_BF16)
    else:
        q_ref[...] = qn.astype(_BF16)
        k_ref[...] = kn.astype(_BF16)
        kf_ref[...] = kn
        vf_ref[...] = v
    v_ref[...] = v.astype(_BF16)

    if even:
        base = wq + 2 * wk
        ua = _gelu_tanh(z[:, base:base + B_WIDTH])
        va = _gelu_tanh(z[:, base + B_WIDTH:base + 2 * B_WIDTH])
        vn = (_rms(va) * vg_ref[...]).astype(_BF16)
        n_chunks = tm // CHUNK
        for g in range(B_GROUPS):
            lanes = slice(g * B_GROUP_DIM, (g + 1) * B_GROUP_DIM)
            rhs = jnp.concatenate([vn[j * CHUNK:(j + 1) * CHUNK, lanes] for j in range(n_chunks)], axis=1)
            mixed = _dot(ws_ref[g], rhs)
            for j in range(n_chunks):
                rows = slice(j * CHUNK, (j + 1) * CHUNK)
                gate_ref[rows, lanes] = (ua[rows, lanes] * (mixed[:, j * CHUNK:(j + 1) * CHUNK] + bs_ref[g])).astype(_BF16)


def _pre(x, mod, w_in, head_mean, q_gain, k_gain, spatial, rope, *, even, seq_len, tm):
    latent = rope is not None
    n_tok = x.shape[0]
    n_in = w_in.shape[1]
    wq = (A_HEADS if even else C_HEADS) * HEAD_DIM
    wk = (A_KV if even else C_KV) * HEAD_DIM
    tiles_per_seq = max(seq_len // tm, 1)
    seqs_per_tile = max(tm // seq_len, 1)
    mod_idx = (lambda i: (i // tiles_per_seq, 0, 0)) if latent else (lambda i: (0, 0, 0))
    del seqs_per_tile
    full = lambda shape: pl.BlockSpec(shape, lambda i: (0,) * len(shape))
    row = lambda w: pl.BlockSpec((tm, w), lambda i: (i, 0))

    args = [x, mod, w_in, head_mean, q_gain, k_gain]
    in_specs = [row(D_MODEL), pl.BlockSpec((1, 6, D_MODEL), mod_idx), full((D_MODEL, n_in)),
                full(head_mean.shape), full((1, wq)), full((1, wk))]
    if even:
        v_gain, w_s, b_s = spatial
        args += [v_gain, w_s, b_s]
        in_specs += [full((1, B_WIDTH)), full(w_s.shape), full(b_s.shape)]
    if latent:
        cos, sin_signed = rope
        pos_idx = lambda i: (i % tiles_per_seq, 0)
        args += [cos, sin_signed]
        in_specs += [pl.BlockSpec((tm, 128), pos_idx), pl.BlockSpec((tm, 128), pos_idx)]

    out_shape = [jax.ShapeDtypeStruct((n_tok, wq), _BF16), jax.ShapeDtypeStruct((n_tok, wk), _BF16),
                 jax.ShapeDtypeStruct((n_tok, wk), _BF16)]
    out_specs = [row(wq), row(wk), row(wk)]
    if even:
        out_shape.append(jax.ShapeDtypeStruct((n_tok, B_WIDTH), _BF16))
        out_specs.append(row(B_WIDTH))
    if not latent:
        out_shape += [jax.ShapeDtypeStruct((n_tok, wk), _F32)] * 2
        out_specs += [row(wk), row(wk)]

    return pl.pallas_call(
        functools.partial(_pre_kernel, even=even, latent=latent, tm=tm),
        grid=(n_tok // tm,),
        in_specs=in_specs,
        out_specs=out_specs,
        out_shape=out_shape,
        compiler_params=pltpu.CompilerParams(vmem_limit_bytes=40 * MIB),
        name=f"pre_{'even' if even else 'odd'}_{'lat' if latent else 'ctx'}",
    )(*args)


def _attn_kernel(*refs, n_kv, tq, seq_len, window, has_ctx, has_sink):
    it = iter(refs)
    if has_sink:
        sink_ref = next(it)
    q_ref, k_ref, v_ref = (next(it) for _ in range(3))
    if has_ctx:
        kc_ref, vc_ref = (next(it) for _ in range(2))
    o_ref = next(it)

    rows = GROUP * tq
    if window:
        i = pl.program_id(1)
        span = tq + 2 * WINDOW
        start = pl.multiple_of(jnp.clip(i * tq - WINDOW, 0, seq_len - span), 128)
        k_self = k_ref[pl.ds(start, span), :]
        v_self = v_ref[pl.ds(start, span), :]
        qpos = i * tq + (lax.broadcasted_iota(jnp.int32, (rows, span), 0) & (tq - 1))
        kpos = start + lax.broadcasted_iota(jnp.int32, (rows, span), 1)
        band = jnp.abs(kpos - qpos) <= WINDOW
    else:
        k_self = k_ref[...]
        v_self = v_ref[...]
    if has_ctx:
        k_ctx = kc_ref[...]
        v_ctx = vc_ref[...]
    q = q_ref[...]
    if has_sink:
        head_of_row = lax.broadcasted_iota(jnp.int32, (rows, 1), 0) // tq

    for h in range(n_kv):
        hd = slice(h * HEAD_DIM, (h + 1) * HEAD_DIM)
        qh = q[:, h * GROUP_W:(h + 1) * GROUP_W]
        qs = jnp.concatenate([qh[:, g * HEAD_DIM:(g + 1) * HEAD_DIM] for g in range(GROUP)], axis=0)
        s1 = _dot_nt(qs, k_self[:, hd])
        if window:
            s1 = jnp.where(band, s1, NEG)
        mx = jnp.max(s1, axis=-1, keepdims=True)
        if has_ctx:
            s2 = _dot_nt(qs, k_ctx[:, hd])
            mx = jnp.maximum(mx, jnp.max(s2, axis=-1, keepdims=True))
        if has_sink:
            sk = jnp.full((rows, 1), sink_ref[h * GROUP + GROUP - 1], _F32)
            for g in range(GROUP - 1):
                sk = jnp.where(head_of_row == g, sink_ref[h * GROUP + g], sk)
            mx = jnp.maximum(mx, sk)
        p1 = jnp.exp(s1 - mx)
        den = jnp.sum(p1, axis=-1, keepdims=True)
        o = _dot(p1.astype(_BF16), v_self[:, hd])
        if has_ctx:
            p2 = jnp.exp(s2 - mx)
            den = den + jnp.sum(p2, axis=-1, keepdims=True)
            o = o + _dot(p2.astype(_BF16), v_ctx[:, hd])
        if has_sink:
            den = den + jnp.exp(sk - mx)
        o = o * (1.0 / den)
        o_ref[:, h * GROUP_W:(h + 1) * GROUP_W] = jnp.concatenate(
            [o[g * tq:(g + 1) * tq] for g in range(GROUP)], axis=1).astype(_BF16)


def _attention(q, k, v, ctx_kv, sink, *, n_kv, seq_len, tq, window, n_ctx=None):
    n_tok, wq = q.shape
    wk = k.shape[1]
    n_seq = n_tok // seq_len
    nq = seq_len // tq
    has_ctx = ctx_kv is not None
    has_sink = sink is not None
    args, in_specs = [], []
    if has_sink:
        args.append(sink)
        in_specs.append(pl.BlockSpec(memory_space=pltpu.SMEM))
    seq_spec = pl.BlockSpec((seq_len, wk), lambda b, i: (b, 0))
    args += [q, k, v]
    in_specs += [pl.BlockSpec((tq, wq), lambda b, i: (b * nq + i, 0)), seq_spec, seq_spec]
    if has_ctx:
        ctx_spec = pl.BlockSpec((n_ctx, wk), lambda b, i: (b, 0))
        args += list(ctx_kv)
        in_specs += [ctx_spec, ctx_spec]
    return pl.pallas_call(
        functools.partial(_attn_kernel, n_kv=n_kv, tq=tq, seq_len=seq_len, window=window,
                          has_ctx=has_ctx, has_sink=has_sink),
        grid=(n_seq, nq),
        in_specs=in_specs,
        out_specs=pl.BlockSpec((tq, wq), lambda b, i: (b * nq + i, 0)),
        out_shape=jax.ShapeDtypeStruct((n_tok, wq), _BF16),
        compiler_params=pltpu.CompilerParams(vmem_limit_bytes=48 * MIB),
        name=f"attn_kv{n_kv}_{'win' if window else 'dense'}_{'lat' if has_ctx else 'ctx'}",
    )(*args)


def _post_kernel(*refs, even, halo, tm, seq_len):
    it = iter(refs)

    def take():
        c = next(it)
        return (c, next(it), next(it)) if halo else (c,)

    x_refs = take()
    a_refs = take()
    g_refs = take() if even else None
    mod_ref, wo_ref, wu_ref, cw_ref, cb_ref, wd_ref, o_ref, h_ref, x1_ref, acc_ref = (next(it) for _ in range(10))

    pad = HALO if halo else 0
    te = tm + 2 * pad

    def ext(rs):
        if not halo:
            return rs[0][...]
        c, p, n = rs
        return jnp.concatenate([p[...], c[...], n[...]], axis=0)

    m = mod_ref[0]
    mix = ext(a_refs)
    if even:
        mix = jnp.concatenate([mix, ext(g_refs)], axis=1)
    x1 = ext(x_refs) + m[2:3] * _dot(mix, wo_ref[...])
    h2 = _rms(x1) * (1.0 + m[4:5]) + m[3:4]
    row = lax.broadcasted_iota(jnp.int32, (te, 1), 0)
    if halo:
        tiles_per_seq = seq_len // tm
        t = pl.program_id(0) % tiles_per_seq
        keep = ((row >= pad) | (t != 0)) & ((row < pad + tm) | (t != tiles_per_seq - 1))
        h2 = jnp.where(keep, h2, 0.0)
    h_ref[...] = h2.astype(_BF16)
    x1_ref[...] = x1
    acc_ref[...] = jnp.zeros_like(acc_ref)

    def chunk(j, carry):
        z = _dot(h_ref[...], wu_ref[j])
        z_prev = pltpu.roll(z, 1, 0)
        z_next = pltpu.roll(z, te - 1, 0)
        if not halo:
            pos = lax.broadcasted_iota(jnp.int32, z.shape, 0) & (seq_len - 1)
            z_prev = jnp.where(pos != 0, z_prev, 0.0)
            z_next = jnp.where(pos != seq_len - 1, z_next, 0.0)
        cw = cw_ref[j]
        zc = cb_ref[j] + (cw[0:1] * z_prev + cw[1:2] * z + cw[2:3] * z_next)
        a = zc[:, :FFN_CHUNK]
        g = zc[:, FFN_CHUNK:]
        act = a * (g * _sigmoid(g))
        acc_ref[...] += _dot(act.astype(_BF16), wd_ref[j])
        return carry

    lax.fori_loop(0, N_FFN_CHUNKS, chunk, 0)
    out = x1_ref[...] + m[5:6] * acc_ref[...]
    o_ref[...] = out[pad:pad + tm]


def _post(x, attn, gate, mod, w_out, w_up, conv_w, conv_b, w_down, *, seq_len, tm, halo):
    even = gate is not None
    latent = mod.shape[0] > 1
    n_tok = x.shape[0]
    n_halo_blocks = n_tok // HALO
    per = tm // HALO
    tiles_per_seq = max(seq_len // tm, 1)
    mod_idx = (lambda i: (i // tiles_per_seq, 0, 0)) if latent else (lambda i: (0, 0, 0))
    const = lambda shape: pl.BlockSpec(shape, lambda i: (0,) * len(shape), pipeline_mode=pl.Buffered(1))

    def tiled(arr):
        w = arr.shape[1]
        specs = [pl.BlockSpec((tm, w), lambda i: (i, 0))]
        if halo:
            specs.append(pl.BlockSpec((HALO, w), lambda i: (jnp.maximum(i * per - 1, 0), 0)))
            specs.append(pl.BlockSpec((HALO, w), lambda i: (jnp.minimum((i + 1) * per, n_halo_blocks - 1), 0)))
        return [arr] * len(specs), specs

    args, in_specs = [], []
    for arr in (x, attn) + ((gate,) if even else ()):
        a, s = tiled(arr)
        args += a
        in_specs += s
    args += [mod, w_out, w_up, conv_w, conv_b, w_down]
    in_specs += [pl.BlockSpec((1, 6, D_MODEL), mod_idx), const(w_out.shape), const(w_up.shape),
                 const(conv_w.shape), const(conv_b.shape), const(w_down.shape)]
    te = tm + (2 * HALO if halo else 0)
    return pl.pallas_call(
        functools.partial(_post_kernel, even=even, halo=halo, tm=tm, seq_len=seq_len),
        grid=(n_tok // tm,),
        in_specs=in_specs,
        out_specs=pl.BlockSpec((tm, D_MODEL), lambda i: (i, 0)),
        out_shape=jax.ShapeDtypeStruct((n_tok, D_MODEL), _F32),
        scratch_shapes=[pltpu.VMEM((te, D_MODEL), _BF16), pltpu.VMEM((te, D_MODEL), _F32),
                        pltpu.VMEM((te, D_MODEL), _F32)],
        compiler_params=pltpu.CompilerParams(vmem_limit_bytes=56 * MIB),
        name=f"post_{'even' if even else 'odd'}_{'lat' if latent else 'ctx'}",
    )(*args)


def _rope_tables(n_tokens):
    n_rows = n_tokens // GRID_W
    rows = jnp.repeat(jnp.arange(n_rows), GRID_W).astype(_F32)
    cols = jnp.tile(jnp.arange(GRID_W), n_rows).astype(_F32)
    n_freq = HEAD_DIM // 4
    inv = ROPE_THETA ** (-jnp.arange(n_freq, dtype=_F32) / n_freq)
    ang = jnp.concatenate([rows[:, None] * inv, cols[:, None] * inv], axis=-1)
    cos = jnp.repeat(jnp.cos(ang), 2, axis=-1)
    sin = jnp.repeat(jnp.sin(ang), 2, axis=-1)
    sign = jnp.tile(jnp.array([-1.0, 1.0], _F32), HEAD_DIM // 2)
    return jnp.tile(cos, (1, 2)), jnp.tile(sin * sign, (1, 2))


def kernel(x_prompt, x_sample, cache_k_attn, cache_v_attn, cache_k_swa, cache_v_swa, c, c_ctx, w_mod, b_mod, w_in_even, q_norm_a, k_norm_a, v_norm_b, w_spatial, b_spatial, w_out_even, w_in_odd, q_norm_c, k_norm_c, sink_c, w_out_odd, w_up, conv_w, conv_b, w_down):
    batch, seq, _ = x_prompt.shape
    dec_batch, dec_seq, _ = x_sample.shape
    past = cache_k_attn.shape[2]
    depth = w_mod.shape[0]
    scale = HEAD_DIM ** -0.5

    cond = jnp.concatenate([c_ctx[None, :], c, jnp.zeros((N_MOD_ROWS - 1 - dec_batch, D_MODEL), _F32)], axis=0)
    mod = _modulation(cond, w_mod, b_mod)
    rope = _rope_tables(dec_seq)
    lane = jnp.arange(GROUP_W)
    head_mean = jnp.where(lane[:, None] // HEAD_DIM == lane[None, :] // HEAD_DIM, 1.0 / HEAD_DIM, 0.0).astype(_BF16)

    xc = x_prompt.reshape(batch * seq, D_MODEL)
    xs = x_sample.reshape(dec_batch * dec_seq, D_MODEL)
    new_kv = []
    for layer in range(depth):
        j = layer // 2
        even = layer % 2 == 0
        mod_c, mod_s = mod[layer, 0:1], mod[layer, 1:1 + dec_batch]
        wu = w_up[layer].astype(_BF16).reshape(D_MODEL, 2, N_FFN_CHUNKS, FFN_CHUNK).transpose(2, 0, 1, 3)
        wu = wu.reshape(N_FFN_CHUNKS, D_MODEL, 2 * FFN_CHUNK)
        cw = conv_w[layer].reshape(3, 2, N_FFN_CHUNKS, FFN_CHUNK).transpose(2, 0, 1, 3).reshape(N_FFN_CHUNKS, 3, 2 * FFN_CHUNK)
        cb = conv_b[layer].reshape(2, N_FFN_CHUNKS, FFN_CHUNK).transpose(1, 0, 2).reshape(N_FFN_CHUNKS, 1, 2 * FFN_CHUNK)
        wd = w_down[layer].astype(_BF16).reshape(N_FFN_CHUNKS, FFN_CHUNK, D_MODEL)
        if even:
            n_q, n_kv = A_HEADS, A_KV
            w_in, w_out = w_in_even[j].astype(_BF16), w_out_even[j].astype(_BF16)
            q_gain, k_gain = q_norm_a[j], k_norm_a[j]
            spatial = (v_norm_b[j][None, :], w_spatial[j].astype(_BF16),
                       jnp.broadcast_to(b_spatial[j][:, :, None], (B_GROUPS, CHUNK, B_GROUP_DIM)))
            cache_k, cache_v, sink = cache_k_attn[:, j], cache_v_attn[:, j], None
        else:
            n_q, n_kv = C_HEADS, C_KV
            w_in, w_out = w_in_odd[j].astype(_BF16), w_out_odd[j].astype(_BF16)
            q_gain, k_gain = q_norm_c[j], k_norm_c[j]
            spatial = None
            cache_k, cache_v, sink = cache_k_swa[:, j], cache_v_swa[:, j], sink_c[j]
        q_gain = jnp.tile(q_gain, n_q)[None, :] * scale
        k_gain = jnp.tile(k_gain, n_kv)[None, :]
        ctx_kv = (cache_k.reshape(dec_batch * past, n_kv * HEAD_DIM).astype(_BF16),
                  cache_v.reshape(dec_batch * past, n_kv * HEAD_DIM).astype(_BF16))

        outs_c = _pre(xc, mod_c, w_in, head_mean, q_gain, k_gain, spatial, None, even=even, seq_len=seq, tm=512)
        outs_s = _pre(xs, mod_s, w_in, head_mean, q_gain, k_gain, spatial, rope, even=even, seq_len=dec_seq, tm=512)
        if even:
            qc, kc, vc, gate_c, kf, vf = outs_c
            qs, ks, vs, gate_s = outs_s
        else:
            qc, kc, vc, kf, vf = outs_c
            qs, ks, vs = outs_s
            gate_c = gate_s = None
        new_kv.append((kf.reshape(batch, seq, n_kv, HEAD_DIM), vf.reshape(batch, seq, n_kv, HEAD_DIM)))

        attn_c = _attention(qc, kc, vc, None, sink, n_kv=n_kv, seq_len=seq, tq=seq, window=False)
        attn_s = _attention(qs, ks, vs, ctx_kv, sink, n_kv=n_kv, seq_len=dec_seq, tq=128,
                            window=not even, n_ctx=past)

        ffn = (w_out, wu, cw, cb, wd)
        xc = _post(xc, attn_c, gate_c, mod_c, *ffn, seq_len=seq, tm=512, halo=False)
        xs = _post(xs, attn_s, gate_s, mod_s, *ffn, seq_len=dec_seq, tm=512, halo=True)

    new_k_attn = jnp.stack([new_kv[l][0] for l in range(0, depth, 2)], axis=1)
    new_v_attn = jnp.stack([new_kv[l][1] for l in range(0, depth, 2)], axis=1)
    new_k_swa = jnp.stack([new_kv[l][0] for l in range(1, depth, 2)], axis=1)
    new_v_swa = jnp.stack([new_kv[l][1] for l in range(1, depth, 2)], axis=1)
    return (xc.reshape(batch, seq, D_MODEL), xs.reshape(dec_batch, dec_seq, D_MODEL),
            new_k_attn, new_v_attn, new_k_swa, new_v_swa)
```
